```python
import math
import jax, jax.numpy as jnp
from jax import lax
import numpy as np

D_MODEL = 1024
BATCH = 8
SEQ = 4096
DEPTH = 1

M_HEADS = 4
M_WIDTH = D_MODEL
M_HEAD_DIM = M_WIDTH // M_HEADS
M_CONV = 4
M_CHUNK = 64
S_WIDTH = D_MODEL // 2
S_GROUP = 16
S_GROUPS = S_WIDTH // S_GROUP
S_STATE = 64
FFN_DIM = 2816
FFN_CONV = 3
EPS = 1e-6
IN_SIZES = (M_WIDTH, M_WIDTH, M_HEADS, M_HEADS, S_WIDTH, D_MODEL, D_MODEL)
N_IN = sum(IN_SIZES)

kernel_name = "hybrid_mlstm_s5_gated_merge_convffn"


def rmsnorm(x, g):
    x32 = x.astype(jnp.float32)
    y = x32 * lax.rsqrt(jnp.mean(x32 * x32, axis=-1, keepdims=True) + EPS)
    return (y * g.astype(jnp.float32)).astype(x.dtype)


def causal_dwconv(x, w, b):
    width = w.shape[0]
    L = x.shape[1]
    xp = jnp.pad(x, ((0, 0), (width - 1, 0), (0, 0)))
    return sum(xp[:, k:k + L] * w[k] for k in range(width)) + b


def mlstm_chunkwise(q, k, v, ig, fg):
    Bsz, L, H, Dh = q.shape
    nc = L // M_CHUNK

    def chunks4(t):
        return t.reshape(Bsz, nc, M_CHUNK, H, Dh).transpose(1, 0, 3, 2, 4)

    def chunks3(t):
        return t.reshape(Bsz, nc, M_CHUNK, H).transpose(1, 0, 3, 2)

    li = ig
    lf = jax.nn.log_sigmoid(fg)
    causal = jnp.tril(jnp.ones((M_CHUNK, M_CHUNK), dtype=bool))

    def step(carry, inp):
        C, n, m = carry
        qc, kc, vc, lic, lfc = inp
        b = jnp.cumsum(lfc, axis=-1)
        dmat = jnp.where(causal, b[..., :, None] - b[..., None, :] + lic[..., None, :], -jnp.inf)
        inter = b + m[..., None]
        m_row = jnp.maximum(inter, jnp.max(dmat, axis=-1))
        w_intra = jnp.exp(dmat - m_row[..., None])
        w_inter = jnp.exp(inter - m_row)
        s = jnp.einsum('bhqd,bhkd->bhqk', qc, kc) * w_intra
        num = (w_inter[..., None] * jnp.einsum('bhvd,bhqd->bhqv', C, qc)
               + jnp.einsum('bhqk,bhkv->bhqv', s, vc))
        den = w_inter * jnp.einsum('bhd,bhqd->bhq', n, qc) + jnp.sum(s, axis=-1)
        h = num / jnp.maximum(jnp.abs(den), jnp.exp(-m_row))[..., None]
        b_last = b[..., -1]
        g = b_last[..., None] - b + lic
        m_new = jnp.maximum(b_last + m, jnp.max(g, axis=-1))
        decay = jnp.exp(b_last + m - m_new)
        wk = jnp.exp(g - m_new[..., None])
        C_new = decay[..., None, None] * C + jnp.einsum('bhc,bhcv,bhcd->bhvd', wk, vc, kc)
        n_new = decay[..., None] * n + jnp.einsum('bhc,bhcd->bhd', wk, kc)
        return (C_new, n_new, m_new), h

    init = (jnp.zeros((Bsz, H, Dh, Dh), jnp.float32),
            jnp.zeros((Bsz, H, Dh), jnp.float32),
            jnp.zeros((Bsz, H), jnp.float32))
    _, hs = lax.scan(step, init, (chunks4(q), chunks4(k), chunks4(v), chunks3(li), chunks3(lf)))
    return hs.transpose(1, 0, 3, 2, 4).reshape(Bsz, L, H, Dh)


def mlstm_mixer(xm, om, ig, fg, conv_w, conv_b, wq, wk, wv, head_g, skip):
    Bsz, L, _ = xm.shape
    xc = jax.nn.silu(causal_dwconv(xm, conv_w, conv_b))
    xc_h = xc.reshape(Bsz, L, M_HEADS, M_HEAD_DIM)
    xm_h = xm.reshape(Bsz, L, M_HEADS, M_HEAD_DIM)
    q = jnp.einsum('blhd,hde->blhe', xc_h, wq).astype(jnp.float32)
    k = (jnp.einsum('blhd,hde->blhe', xc_h, wk) * (M_HEAD_DIM ** -0.5)).astype(jnp.float32)
    v = jnp.einsum('blhd,hde->blhe', xm_h, wv).astype(jnp.float32)
    hc = mlstm_chunkwise(q, k, v, ig.astype(jnp.float32), fg.astype(jnp.float32))
    mu = jnp.mean(hc, axis=-1, keepdims=True)
    var = jnp.mean(jnp.square(hc - mu), axis=-1, keepdims=True)
    hn = ((hc - mu) * lax.rsqrt(var + EPS)).reshape(Bsz, L, M_WIDTH) * head_g.astype(jnp.float32)
    out = jax.nn.sigmoid(om.astype(jnp.float32)) * hn + skip.astype(jnp.float32) * xc.astype(jnp.float32)
    return out.astype(xm.dtype)


def _complex_affine_combine(e1, e2):
    a1r, a1i, b1r, b1i = e1
    a2r, a2i, b2r, b2i = e2
    ar = a1r * a2r - a1i * a2i
    ai = a1r * a2i + a1i * a2r
    br = a2r * b1r - a2i * b1i + b2r
    bi = a2r * b1i + a2i * b1r + b2i
    return (ar, ai, br, bi)


def s5_mixer(u, a_re, a_im, log_dt, b_re, b_im, c_re, c_im, d_skip, w_glu, b_glu):
    Bsz, L, _ = u.shape
    f32 = jnp.float32
    u32 = u.astype(f32).reshape(Bsz, L, S_GROUPS, S_GROUP)
    ar, ai = a_re.astype(f32), a_im.astype(f32)
    dt = jnp.exp(log_dt.astype(f32))[:, None]
    mag = jnp.exp(dt * ar)
    abar_re, abar_im = mag * jnp.cos(dt * ai), mag * jnp.sin(dt * ai)
    den = ar * ar + ai * ai
    xr, xi = abar_re - 1.0, abar_im
    r_re = (xr * ar + xi * ai) / den
    r_im = (xi * ar - xr * ai) / den
    br_, bi_ = b_re.astype(f32), b_im.astype(f32)
    bbar_re = r_re[..., None] * br_ - r_im[..., None] * bi_
    bbar_im = r_re[..., None] * bi_ + r_im[..., None] * br_
    bu_re = jnp.einsum('blgc,gpc->blgp', u32, bbar_re)
    bu_im = jnp.einsum('blgc,gpc->blgp', u32, bbar_im)
    a_seq_re = jnp.broadcast_to(abar_re, (1, L, S_GROUPS, S_STATE))
    a_seq_im = jnp.broadcast_to(abar_im, (1, L, S_GROUPS, S_STATE))
    _, _, s_re, s_im = lax.associative_scan(_complex_affine_combine,
                                            (a_seq_re, a_seq_im, bu_re, bu_im), axis=1)
    y = (jnp.einsum('blgp,gcp->blgc', s_re, c_re.astype(f32))
         - jnp.einsum('blgp,gcp->blgc', s_im, c_im.astype(f32))
         + d_skip.astype(f32) * u32)
    y = jax.nn.gelu(y.reshape(Bsz, L, S_WIDTH))
    y = y * jax.nn.sigmoid(y @ w_glu.astype(f32) + b_glu.astype(f32))
    return y.astype(u.dtype)


def setup_inputs(seed: int = 0) -> dict:
    key = jax.random.key(seed)
    ks = jax.random.split(key, 40)
    nrm = lambda k, s, sc: jax.random.normal(k, s, jnp.float32) * sc
    Ld = DEPTH
    x = jax.random.normal(ks[0], (BATCH, SEQ, D_MODEL), jnp.float32)
    mix_norm_g = 1.0 + nrm(ks[1], (Ld, D_MODEL), 0.02)
    w_in = nrm(ks[2], (Ld, D_MODEL, N_IN), D_MODEL ** -0.5)
    b_in = jnp.concatenate([
        nrm(ks[3], (Ld, 2 * M_WIDTH), 0.02),
        nrm(ks[4], (Ld, M_HEADS), 0.1),
        jnp.linspace(3.0, 6.0, M_HEADS)[None, :] + nrm(ks[5], (Ld, M_HEADS), 0.1),
        nrm(ks[6], (Ld, S_WIDTH + 2 * D_MODEL), 0.02)], axis=-1)
    m_conv_w = nrm(ks[7], (Ld, M_CONV, M_WIDTH), M_CONV ** -0.5)
    m_conv_b = nrm(ks[8], (Ld, M_WIDTH), 0.02)
    m_wq = nrm(ks[9], (Ld, M_HEADS, M_HEAD_DIM, M_HEAD_DIM), M_HEAD_DIM ** -0.5)
    m_wk = nrm(ks[10], (Ld, M_HEADS, M_HEAD_DIM, M_HEAD_DIM), M_HEAD_DIM ** -0.5)
    m_wv = nrm(ks[11], (Ld, M_HEADS, M_HEAD_DIM, M_HEAD_DIM), M_HEAD_DIM ** -0.5)
    m_head_g = 1.0 + nrm(ks[12], (Ld, M_WIDTH), 0.02)
    m_skip = 1.0 + nrm(ks[13], (Ld, M_WIDTH), 0.02)
    s_a_re = -0.5 + nrm(ks[14], (Ld, S_GROUPS, S_STATE), 0.01)
    s_a_im = (math.pi * jnp.arange(S_STATE, dtype=jnp.float32))[None, None, :] + nrm(ks[15], (Ld, S_GROUPS, S_STATE), 0.01)
    s_log_dt = jax.random.uniform(ks[16], (Ld, S_GROUPS), jnp.float32, math.log(1e-3), math.log(1e-1))
    s_b_re = nrm(ks[17], (Ld, S_GROUPS, S_STATE, S_GROUP), (2 * S_GROUP) ** -0.5)
    s_b_im = nrm(ks[18], (Ld, S_GROUPS, S_STATE, S_GROUP), (2 * S_GROUP) ** -0.5)
    s_c_re = nrm(ks[19], (Ld, S_GROUPS, S_GROUP, S_STATE), S_STATE ** -0.5)
    s_c_im = nrm(ks[20], (Ld, S_GROUPS, S_GROUP, S_STATE), S_STATE ** -0.5)
    s_d = nrm(ks[21], (Ld, S_GROUPS, S_GROUP), 1.0)
    s_w_glu = nrm(ks[22], (Ld, S_WIDTH, S_WIDTH), S_WIDTH ** -0.5)
    s_b_glu = nrm(ks[23], (Ld, S_WIDTH), 0.02)
    w_branch_a = nrm(ks[24], (Ld, M_WIDTH, D_MODEL), M_WIDTH ** -0.5)
    w_branch_b = nrm(ks[25], (Ld, S_WIDTH, D_MODEL), S_WIDTH ** -0.5)
    w_out = nrm(ks[26], (Ld, D_MODEL, D_MODEL), D_MODEL ** -0.5)
    ffn_norm_g = 1.0 + nrm(ks[27], (Ld, D_MODEL), 0.02)
    w_up = nrm(ks[28], (Ld, D_MODEL, 2 * FFN_DIM), D_MODEL ** -0.5)
    ffn_conv_w = nrm(ks[29], (Ld, FFN_CONV, 2 * FFN_DIM), FFN_CONV ** -0.5)
    ffn_conv_b = nrm(ks[30], (Ld, 2 * FFN_DIM), 0.02)
    w_down = nrm(ks[31], (Ld, FFN_DIM, D_MODEL), FFN_DIM ** -0.5)
    final_norm_g = 1.0 + nrm(ks[32], (D_MODEL,), 0.02)
    return {"x": x, "mix_norm_g": mix_norm_g, "w_in": w_in, "b_in": b_in,
            "m_conv_w": m_conv_w, "m_conv_b": m_conv_b, "m_wq": m_wq, "m_wk": m_wk, "m_wv": m_wv,
            "m_head_g": m_head_g, "m_skip": m_skip,
            "s_a_re": s_a_re, "s_a_im": s_a_im, "s_log_dt": s_log_dt, "s_b_re": s_b_re, "s_b_im": s_b_im,
            "s_c_re": s_c_re, "s_c_im": s_c_im, "s_d": s_d, "s_w_glu": s_w_glu, "s_b_glu": s_b_glu,
            "w_branch_a": w_branch_a, "w_branch_b": w_branch_b, "w_out": w_out,
            "ffn_norm_g": ffn_norm_g, "w_up": w_up, "ffn_conv_w": ffn_conv_w, "ffn_conv_b": ffn_conv_b,
            "w_down": w_down, "final_norm_g": final_norm_g}


def reference(x, mix_norm_g, w_in, b_in, m_conv_w, m_conv_b, m_wq, m_wk, m_wv, m_head_g, m_skip,
              s_a_re, s_a_im, s_log_dt, s_b_re, s_b_im, s_c_re, s_c_im, s_d, s_w_glu, s_b_glu,
              w_branch_a, w_branch_b, w_out, ffn_norm_g, w_up, ffn_conv_w, ffn_conv_b, w_down,
              final_norm_g):
    splits = [int(s) for s in np.cumsum(IN_SIZES)[:-1]]
    for l in range(DEPTH):
        h = rmsnorm(x, mix_norm_g[l])
        proj = h @ w_in[l] + b_in[l]
        xm, om, ig, fg, us, ga, gb = jnp.split(proj, splits, axis=-1)
        a_out = mlstm_mixer(xm, om, ig, fg, m_conv_w[l], m_conv_b[l], m_wq[l], m_wk[l], m_wv[l],
                            m_head_g[l], m_skip[l])
        b_out = s5_mixer(us, s_a_re[l], s_a_im[l], s_log_dt[l], s_b_re[l], s_b_im[l],
                         s_c_re[l], s_c_im[l], s_d[l], s_w_glu[l], s_b_glu[l])
        merged = (jax.nn.sigmoid(ga) * (a_out @ w_branch_a[l])
                  + jax.nn.sigmoid(gb) * (b_out @ w_branch_b[l]))
        x = x + merged @ w_out[l]
        hf = rmsnorm(x, ffn_norm_g[l])
        up = causal_dwconv(hf @ w_up[l], ffn_conv_w[l], ffn_conv_b[l])
        val, gate = jnp.split(up, 2, axis=-1)
        x = x + (jax.nn.silu(gate) * val) @ w_down[l]
    return rmsnorm(x, final_norm_g)
```

```python
import functools

import jax
import jax.numpy as jnp
from jax import lax
from jax.experimental import pallas as pl
from jax.experimental.pallas import tpu as pltpu

F32 = jnp.float32
BF16 = jnp.bfloat16
EPS = 1e-6

M_HEADS = 4
M_CONV = 4
S_GROUP = 16
S_STATE = 64
FFN_CONV = 3

SUBLANES = 8
LANES = 128
MXU_DIM = 256
VMEM_LIMIT_BYTES = 56 * 1024 * 1024

INPROJ_ROWS = 512
S5_STEPS = 64
S5_COLS = 512
MLSTM_CHUNK = 256
FFN_ROWS = 256
FFN_COLS = 512


def _const_spec(shape):
    nd = len(shape)
    return pl.BlockSpec(shape, lambda *_: (0,) * nd, pipeline_mode=pl.Buffered(1))


def _params(n_axes):
    return pltpu.CompilerParams(dimension_semantics=("arbitrary",) * n_axes,
                                vmem_limit_bytes=VMEM_LIMIT_BYTES)


def _dot(a, b):
    return jnp.dot(a, b, preferred_element_type=F32)


def _sigmoid(x):
    return 1.0 / (1.0 + jnp.exp(-x))


def _rmsnorm(x, g):
    return x * lax.rsqrt(jnp.mean(x * x, axis=-1, keepdims=True) + EPS) * g


def _s5_prep_kernel(ar_ref, ai_ref, ldt_ref, bre_ref, bim_ref,
                    abr_ref, abi_ref, bbr_ref, bbi_ref):
    ar, ai = ar_ref[...], ai_ref[...]
    dt = jnp.exp(ldt_ref[...])
    mag = jnp.exp(dt * ar)
    abar_re = mag * jnp.cos(dt * ai)
    abar_im = mag * jnp.sin(dt * ai)
    den = ar * ar + ai * ai
    xr, xi = abar_re - 1.0, abar_im
    r_re = (xr * ar + xi * ai) / den
    r_im = (xi * ar - xr * ai) / den
    br, bi = bre_ref[...], bim_ref[...]
    abr_ref[...] = abar_re
    abi_ref[...] = abar_im
    bbr_ref[...] = r_re * br - r_im * bi
    bbi_ref[...] = r_re * bi + r_im * br


def _s5_prep(a_re, a_im, log_dt, b_re, b_im):
    G, P = a_re.shape
    C = b_re.shape[-1]
    rep = lambda a: jnp.repeat(a, C, axis=0)
    ar, ai = rep(a_re), rep(a_im)
    ldt = rep(jnp.broadcast_to(log_dt[:, None], (G, P)))
    bt = lambda b: b.transpose(0, 2, 1).reshape(G * C, P)
    shp = jax.ShapeDtypeStruct((G * C, P), F32)
    abr, abi, bbr, bbi = pl.pallas_call(
        _s5_prep_kernel, out_shape=(shp, shp, shp, shp), name="s5_prep",
    )(ar, ai, ldt, bt(b_re), bt(b_im))
    eye = jnp.eye(G, dtype=F32)
    blockdiag = lambda m: (m.reshape(G, C, 1, P) * eye.reshape(G, 1, G, 1)).reshape(G * C, G * P)
    flat = lambda m: m[::C].reshape(1, G * P)
    return flat(abr), flat(abi), blockdiag(bbr).astype(BF16), blockdiag(bbi).astype(BF16)


def _inproj_kernel(x_ref, g_ref, wm_ref, bm_ref, wg_ref, bg_ref,
                   xm_ref, om_ref, us_ref, ga_ref, gb_ref, gt_ref, *, widths):
    hb = _rmsnorm(x_ref[0], g_ref[...]).astype(BF16)
    outs = (xm_ref, om_ref, us_ref, ga_ref, gb_ref)
    col = 0
    for o_ref, w in zip(outs, widths):
        y = _dot(hb, wm_ref[:, col:col + w]) + bm_ref[:, col:col + w]
        if len(o_ref.shape) == 3:
            o_ref[0] = y.astype(o_ref.dtype)
        else:
            o_ref[...] = y.astype(o_ref.dtype)
        col += w
    gt_ref[0] = _dot(hb, wg_ref[...]) + bg_ref[...]


def _inproj(x, norm_g, w_main, b_main, w_gate, b_gate, widths):
    B, L, D = x.shape
    tm = min(INPROJ_ROWS, L)
    wm, wo, wu, wa, wb = widths
    tok = lambda w: pl.BlockSpec((1, tm, w), lambda b, l: (b, l, 0))
    out_shape = (
        jax.ShapeDtypeStruct((B, L, wm), BF16),
        jax.ShapeDtypeStruct((B, L, wo), BF16),
        jax.ShapeDtypeStruct((L, B * wu), BF16),
        jax.ShapeDtypeStruct((B, L, wa), BF16),
        jax.ShapeDtypeStruct((B, L, wb), BF16),
        jax.ShapeDtypeStruct((B, L, LANES), F32),
    )
    out_specs = (tok(wm), tok(wo), pl.BlockSpec((tm, wu), lambda b, l: (l, b)),
                 tok(wa), tok(wb), tok(LANES))
    return pl.pallas_call(
        functools.partial(_inproj_kernel, widths=widths),
        grid=(B, L // tm),
        in_specs=[tok(D), _const_spec((1, D)), _const_spec(w_main.shape), _const_spec(b_main.shape),
                  _const_spec(w_gate.shape), _const_spec(b_gate.shape)],
        out_specs=out_specs, out_shape=out_shape,
        compiler_params=_params(2), name="inproj",
    )(x, norm_g, w_main, b_main, w_gate, b_gate)


def _gelu_tanh(x):
    return 0.5 * x * (1.0 + jnp.tanh(0.7978845608028654 * (x + 0.044715 * x * x * x)))


def _s5_kernel(u_ref, bre_ref, bim_ref, are_ref, aim_ref, cre_ref, cim_ref, d_ref,
               wglu_ref, bglu_ref, o_ref, sr_ref, si_ref, str_ref, sti_ref, *, group_cols):
    rows, width = u_ref.shape
    n_state = sr_ref.shape[1]
    steps = rows // SUBLANES

    @pl.when(pl.program_id(0) == 0)
    def _():
        str_ref[...] = jnp.zeros_like(str_ref)
        sti_ref[...] = jnp.zeros_like(sti_ref)

    u = u_ref[...]
    k_w = LANES
    n_w = n_state * k_w // width
    for q in range(width // k_w):
        uq = u[:, q * k_w:(q + 1) * k_w]
        sr_ref[:, q * n_w:(q + 1) * n_w] = _dot(uq, bre_ref[q * k_w:(q + 1) * k_w, q * n_w:(q + 1) * n_w])
        si_ref[:, q * n_w:(q + 1) * n_w] = _dot(uq, bim_ref[q * k_w:(q + 1) * k_w, q * n_w:(q + 1) * n_w])

    for cb in range(n_state // S5_COLS):
        cs = slice(cb * S5_COLS, (cb + 1) * S5_COLS)
        a_r = jnp.broadcast_to(are_ref[:, cs], (SUBLANES, S5_COLS))
        a_i = jnp.broadcast_to(aim_ref[:, cs], (SUBLANES, S5_COLS))

        def body(t, carry, cs=cs, a_r=a_r, a_i=a_i):
            s_r, s_i = carry
            r0 = pl.multiple_of(t * SUBLANES, SUBLANES)
            n_r = a_r * s_r - a_i * s_i + sr_ref[pl.ds(r0, SUBLANES), cs]
            n_i = a_r * s_i + a_i * s_r + si_ref[pl.ds(r0, SUBLANES), cs]
            sr_ref[pl.ds(r0, SUBLANES), cs] = n_r
            si_ref[pl.ds(r0, SUBLANES), cs] = n_i
            return n_r, n_i

        s_r, s_i = lax.fori_loop(0, steps, body, (str_ref[:, cs], sti_ref[:, cs]), unroll=8)
        str_ref[:, cs] = s_r
        sti_ref[:, cs] = s_i

    n_out = MXU_DIM
    k_c = n_state * n_out // width
    halves = []
    for h in range(width // n_out):
        ks = slice(h * k_c, (h + 1) * k_c)
        ns = slice(h * n_out, (h + 1) * n_out)
        halves.append(_dot(sr_ref[:, ks].astype(BF16), cre_ref[ks, ns])
                      - _dot(si_ref[:, ks].astype(BF16), cim_ref[ks, ns]))
    y = jnp.concatenate(halves, axis=-1) + d_ref[...] * u.astype(F32)
    y = _gelu_tanh(y)
    z = _dot(y.astype(BF16), wglu_ref[...]) + bglu_ref[...]
    o_ref[...] = (y * _sigmoid(z)).astype(o_ref.dtype)


def _s5_mixer(u_tm, bbr, bbi, abr, abi, cbr, cbi, d_row, w_glu, b_glu):
    rows_total, width = u_tm.shape
    n_state = abr.shape[1]
    rows = S5_STEPS * SUBLANES
    blk = pl.BlockSpec((rows, width), lambda i: (i, 0))
    consts = (bbr, bbi, abr, abi, cbr, cbi, d_row, w_glu, b_glu)
    return pl.pallas_call(
        functools.partial(_s5_kernel, group_cols=S_STATE),
        grid=(rows_total // rows,),
        in_specs=[blk] + [_const_spec(c.shape) for c in consts],
        out_specs=blk,
        out_shape=jax.ShapeDtypeStruct((rows_total, width), BF16),
        scratch_shapes=[pltpu.VMEM((rows, n_state), F32), pltpu.VMEM((rows, n_state), F32),
                        pltpu.VMEM((SUBLANES, n_state), F32), pltpu.VMEM((SUBLANES, n_state), F32)],
        compiler_params=_params(1), name="s5_mixer",
    )(u_tm, *consts)


def _log_sigmoid(x):
    return jnp.minimum(x, 0.0) - jnp.log(1.0 + jnp.exp(-jnp.abs(x)))


def _mlstm_kernel(xm_ref, om_ref, gt_ref, ga_ref, gb_ref, bo_ref, x_ref,
                  cw_ref, cb_ref, wq_ref, wk_ref, wv_ref, hg_ref, sk_ref, wa_ref, wb_ref, wo_ref,
                  o_ref, xbuf_ref, ct_ref, n_ref, m_ref, aout_ref):
    lc = xm_ref.shape[1]
    dh = wq_ref.shape[1]
    n_heads = wq_ref.shape[0]
    pad = SUBLANES

    @pl.when(pl.program_id(1) == 0)
    def _():
        xbuf_ref[0:pad, :] = jnp.zeros((pad, xbuf_ref.shape[1]), F32)
        ct_ref[...] = jnp.zeros_like(ct_ref)
        n_ref[...] = jnp.zeros_like(n_ref)
        m_ref[...] = jnp.zeros_like(m_ref)

    xm = xm_ref[0].astype(F32)
    xbuf_ref[pad:pad + lc, :] = xm
    acc = cb_ref[...] + cw_ref[M_CONV - 1:M_CONV, :] * xm
    for d in range(1, M_CONV):
        acc = acc + cw_ref[M_CONV - 1 - d:M_CONV - d, :] * xbuf_ref[pad - d:pad - d + lc, :]
    xbuf_ref[0:pad, :] = xm[lc - pad:lc, :]
    xc = acc * _sigmoid(acc)
    xcb = xc.astype(BF16)
    xmb = xm_ref[0]

    gt = gt_ref[0]
    lf = _log_sigmoid(gt)
    rid = lax.broadcasted_iota(jnp.int32, (lc, lc), 0)
    cid = lax.broadcasted_iota(jnp.int32, (lc, lc), 1)
    causal = cid <= rid
    tril = jnp.where(causal, 1.0, 0.0).astype(BF16)
    lf_hi = lf.astype(BF16)
    lf_lo = (lf - lf_hi.astype(F32)).astype(BF16)
    bc = _dot(tril, lf_hi) + _dot(tril, lf_lo)
    gt_t = gt.T
    bc_t = bc.T

    scale = dh ** -0.5
    for h in range(n_heads):
        hs = slice(h * dh, (h + 1) * dh)
        q = _dot(xcb[:, hs], wq_ref[h])
        k = _dot(xcb[:, hs], wk_ref[h]) * scale
        v = _dot(xmb[:, hs], wv_ref[h])
        qb, vb = q.astype(BF16), v.astype(BF16)

        li_row = gt_t[h:h + 1, :]
        b_row = bc_t[n_heads + h:n_heads + h + 1, :]
        li_col = gt[:, h:h + 1]
        b_col = bc[:, n_heads + h:n_heads + h + 1]
        m_prev = m_ref[h:h + 1, 0:1]

        dmat = jnp.where(causal, b_col + (li_row - b_row), -jnp.inf)
        inter = b_col + m_prev
        m_row = jnp.maximum(inter, jnp.max(dmat, axis=-1, keepdims=True))
        w_intra = jnp.exp(dmat - m_row)
        w_inter = jnp.exp(inter - m_row)
        s = lax.dot_general(qb, k.astype(BF16), (((1,), (1,)), ((), ())),
                            preferred_element_type=F32) * w_intra
        ct = ct_ref[h]
        num = w_inter * _dot(qb, ct.astype(BF16)) + _dot(s.astype(BF16), vb)
        n_row = n_ref[h:h + 1, :]
        den = (w_inter * jnp.sum(q * n_row, axis=-1, keepdims=True)
               + jnp.sum(s, axis=-1, keepdims=True))
        hc = num / jnp.maximum(jnp.abs(den), jnp.exp(-m_row))

        b_last = b_col[lc - 1:lc, :]
        g_col = b_last + (li_col - b_col)
        m_new = jnp.maximum(b_last + m_prev, jnp.max(g_col, axis=0, keepdims=True))
        decay = jnp.exp(b_last + m_prev - m_new)
        kw = k * jnp.exp(g_col - m_new)
        ct_ref[h] = decay * ct + lax.dot_general(kw.astype(BF16), vb, (((0,), (0,)), ((), ())),
                                                 preferred_element_type=F32)
        n_ref[h:h + 1, :] = decay * n_row + jnp.sum(kw, axis=0, keepdims=True)
        m_ref[h:h + 1, :] = jnp.broadcast_to(m_new, (1, m_ref.shape[1]))

        mu = jnp.mean(hc, axis=-1, keepdims=True)
        dv = hc - mu
        var = jnp.mean(dv * dv, axis=-1, keepdims=True)
        hn = dv * lax.rsqrt(var + EPS) * hg_ref[:, hs]
        out_h = _sigmoid(om_ref[0, :, hs].astype(F32)) * hn + sk_ref[:, hs] * xc[:, hs]
        aout_ref[:, hs] = out_h.astype(BF16)

    pa = _dot(aout_ref[...], wa_ref[...])
    pb = _dot(bo_ref[...], wb_ref[...])
    merged = (_sigmoid(ga_ref[0].astype(F32)) * pa + _sigmoid(gb_ref[0].astype(F32)) * pb)
    o_ref[0] = x_ref[0] + _dot(merged.astype(BF16), wo_ref[...])


def _mlstm_merge(xm, om, gt, ga, gb, bo_tm, x, conv_w, conv_b, wq, wk, wv, head_g, skip, w_a, w_b, w_out):
    B, L, D = x.shape
    width = xm.shape[-1]
    ws = bo_tm.shape[1] // B
    lc = min(MLSTM_CHUNK, L)
    n_heads, dh, _ = wq.shape
    tok = lambda w: pl.BlockSpec((1, lc, w), lambda b, c: (b, c, 0))
    consts = (conv_w, conv_b, wq, wk, wv, head_g, skip, w_a, w_b, w_out)
    return pl.pallas_call(
        _mlstm_kernel,
        grid=(B, L // lc),
        in_specs=[tok(width), tok(width), tok(LANES), tok(D), tok(D),
                  pl.BlockSpec((lc, ws), lambda b, c: (c, b)), tok(D)]
                 + [_const_spec(c.shape) for c in consts],
        out_specs=tok(D),
        out_shape=jax.ShapeDtypeStruct((B, L, D), F32),
        scratch_shapes=[pltpu.VMEM((SUBLANES + lc, width), F32),
                        pltpu.VMEM((n_heads, dh, dh), F32),
                        pltpu.VMEM((SUBLANES, dh), F32),
                        pltpu.VMEM((SUBLANES, LANES), F32),
                        pltpu.VMEM((lc, width), BF16)],
        compiler_params=_params(2), name="mlstm_merge",
    )(xm, om, gt, ga, gb, bo_tm, x, *consts)


def _ffn_kernel(x_ref, g_ref, wup_ref, cw_ref, cb_ref, wdn_ref, fg_ref, o_ref,
                up_ref, act_ref, *, final_norm):
    tm = x_ref.shape[1]
    n_up = wup_ref.shape[1]
    f = n_up // 2
    pad = SUBLANES

    @pl.when(pl.program_id(1) == 0)
    def _():
        up_ref[0:pad, :] = jnp.zeros((pad, n_up), F32)

    x = x_ref[0]
    hb = _rmsnorm(x, g_ref[...]).astype(BF16)
    for c0 in range(0, n_up, FFN_COLS):
        up_ref[pad:pad + tm, c0:c0 + FFN_COLS] = _dot(hb, wup_ref[:, c0:c0 + FFN_COLS])

    def conv(c0, w):
        acc = cb_ref[:, c0:c0 + w]
        for d in range(FFN_CONV):
            acc = acc + (cw_ref[FFN_CONV - 1 - d:FFN_CONV - d, c0:c0 + w]
                         * up_ref[pad - d:pad - d + tm, c0:c0 + w])
        return acc

    for c0 in range(0, f, MXU_DIM):
        val = conv(c0, MXU_DIM)
        gate = conv(f + c0, MXU_DIM)
        act_ref[:, c0:c0 + MXU_DIM] = (gate * _sigmoid(gate) * val).astype(BF16)

    up_ref[0:pad, :] = up_ref[tm:tm + pad, :]
    y = x + _dot(act_ref[...], wdn_ref[...])
    if final_norm:
        y = _rmsnorm(y, fg_ref[...])
    o_ref[0] = y


def _ffn(x, norm_g, w_up, conv_w, conv_b, w_down, final_g, final_norm):
    B, L, D = x.shape
    tm = min(FFN_ROWS, L)
    n_up = w_up.shape[1]
    tok = pl.BlockSpec((1, tm, D), lambda b, l: (b, l, 0))
    consts = (norm_g, w_up, conv_w, conv_b, w_down, final_g)
    return pl.pallas_call(
        functools.partial(_ffn_kernel, final_norm=final_norm),
        grid=(B, L // tm),
        in_specs=[tok] + [_const_spec(c.shape) for c in consts],
        out_specs=tok,
        out_shape=jax.ShapeDtypeStruct((B, L, D), F32),
        scratch_shapes=[pltpu.VMEM((SUBLANES + tm, n_up), F32),
                        pltpu.VMEM((tm, n_up // 2), BF16)],
        compiler_params=_params(2), name="ffn",
    )(x, *consts)


def kernel(x, mix_norm_g, w_in, b_in, m_conv_w, m_conv_b, m_wq, m_wk, m_wv, m_head_g, m_skip,
           s_a_re, s_a_im, s_log_dt, s_b_re, s_b_im, s_c_re, s_c_im, s_d, s_w_glu, s_b_glu,
           w_branch_a, w_branch_b, w_out, ffn_norm_g, w_up, ffn_conv_w, ffn_conv_b, w_down,
           final_norm_g):
    B, L, D = x.shape
    assert B == SUBLANES, "s5_mixer lays the batch on the 8 sublanes"
    depth = w_in.shape[0]
    H = m_wq.shape[1]
    mw = m_wq.shape[1] * m_wq.shape[2]
    G, C = s_d.shape[1], s_d.shape[2]
    sw = G * C
    row = lambda v: v.reshape(1, -1).astype(F32)

    for l in range(depth):
        o_ig = 2 * mw
        o_us = o_ig + 2 * H
        o_ga = o_us + sw
        widths = (mw, mw, sw, D, D)
        w_main = jnp.concatenate([w_in[l][:, :o_ig], w_in[l][:, o_us:]], axis=1).astype(BF16)
        b_main = row(jnp.concatenate([b_in[l][:o_ig], b_in[l][o_us:]]))
        w_gate = jnp.pad(w_in[l][:, o_ig:o_us], ((0, 0), (0, LANES - 2 * H))).astype(BF16)
        b_gate = row(jnp.pad(b_in[l][o_ig:o_us], (0, LANES - 2 * H)))

        xm, om, us_tm, ga, gb, gt = _inproj(x, row(mix_norm_g[l]), w_main, b_main, w_gate, b_gate, widths)

        abr, abi, bbr, bbi = _s5_prep(s_a_re[l], s_a_im[l], s_log_dt[l], s_b_re[l], s_b_im[l])
        eye = jnp.eye(G, dtype=F32)
        cdiag = lambda c: (c.transpose(0, 2, 1).reshape(G, S_STATE, 1, C)
                           * eye.reshape(G, 1, G, 1)).reshape(G * S_STATE, sw).astype(BF16)
        bo_tm = _s5_mixer(us_tm.reshape(L * B, sw), bbr, bbi, abr, abi,
                          cdiag(s_c_re[l]), cdiag(s_c_im[l]), row(s_d[l]),
                          s_w_glu[l].astype(BF16), row(s_b_glu[l]))

        x = _mlstm_merge(xm, om, gt, ga, gb, bo_tm.reshape(L, B * sw), x,
                         m_conv_w[l].astype(F32), row(m_conv_b[l]),
                         m_wq[l].astype(BF16), m_wk[l].astype(BF16), m_wv[l].astype(BF16),
                         row(m_head_g[l]), row(m_skip[l]),
                         w_branch_a[l].astype(BF16), w_branch_b[l].astype(BF16), w_out[l].astype(BF16))

        x = _ffn(x, row(ffn_norm_g[l]), w_up[l].astype(BF16), ffn_conv_w[l].astype(F32),
                 row(ffn_conv_b[l]), w_down[l].astype(BF16), row(final_norm_g),
                 final_norm=(l == depth - 1))
    return x
```

```python
import functools

import jax
import jax.numpy as jnp
from jax import lax
from jax.experimental import pallas as pl
from jax.experimental.pallas import tpu as pltpu

F32 = jnp.float32
BF16 = jnp.bfloat16
EPS = 1e-6

SUBLANES = 8
LANES = 128
MXU_DIM = 256
VMEM_LIMIT_BYTES = 56 * 1024 * 1024

INPROJ_ROWS = 512
S5_STEPS = 64
S5_COLS = 512
MLSTM_CHUNK = 256
FFN_ROWS = 256
FFN_COLS = 512


def _const_spec(shape):
    nd = len(shape)
    return pl.BlockSpec(shape, lambda *_: (0,) * nd, pipeline_mode=pl.Buffered(1))


def _params(n_axes):
    return pltpu.CompilerParams(dimension_semantics=("arbitrary",) * n_axes,
                                vmem_limit_bytes=VMEM_LIMIT_BYTES)


def _dot(a, b):
    return jnp.dot(a, b, preferred_element_type=F32)


def _sigmoid(x):
    return 0.5 * jnp.tanh(0.5 * x) + 0.5


def _causal_conv(x, buf_ref, w_ref, b_ref, c0, width):
    n = x.shape[0]
    taps = w_ref.shape[0]
    cs = slice(c0, c0 + width)
    acc = b_ref[:, cs] + w_ref[taps - 1:taps, cs] * x
    for d in range(1, taps):
        acc = acc + w_ref[taps - 1 - d:taps - d, cs] * buf_ref[SUBLANES - d:SUBLANES - d + n, cs]
    return acc


def _rmsnorm(x, g):
    return x * lax.rsqrt(jnp.mean(x * x, axis=-1, keepdims=True) + EPS) * g


def _s5_prep_kernel(ar_ref, ai_ref, ldt_ref, bre_ref, bim_ref,
                    abr_ref, abi_ref, bbr_ref, bbi_ref):
    ar, ai = ar_ref[...], ai_ref[...]
    dt = jnp.exp(ldt_ref[...])
    mag = jnp.exp(dt * ar)
    abar_re = mag * jnp.cos(dt * ai)
    abar_im = mag * jnp.sin(dt * ai)
    den = ar * ar + ai * ai
    xr, xi = abar_re - 1.0, abar_im
    r_re = (xr * ar + xi * ai) / den
    r_im = (xi * ar - xr * ai) / den
    br, bi = bre_ref[...], bim_ref[...]
    abr_ref[...] = abar_re
    abi_ref[...] = abar_im
    bbr_ref[...] = r_re * br - r_im * bi
    bbi_ref[...] = r_re * bi + r_im * br


def _s5_prep(a_re, a_im, log_dt, b_re, b_im):
    G, P = a_re.shape
    C = b_re.shape[-1]
    rep = lambda a: jnp.repeat(a, C, axis=0)
    ar, ai = rep(a_re), rep(a_im)
    ldt = rep(jnp.broadcast_to(log_dt[:, None], (G, P)))
    bt = lambda b: b.transpose(0, 2, 1).reshape(G * C, P)
    shp = jax.ShapeDtypeStruct((G * C, P), F32)
    abr, abi, bbr, bbi = pl.pallas_call(
        _s5_prep_kernel, out_shape=(shp, shp, shp, shp), name="s5_prep",
    )(ar, ai, ldt, bt(b_re), bt(b_im))
    eye = jnp.eye(G, dtype=F32)
    blockdiag = lambda m: (m.reshape(G, C, 1, P) * eye.reshape(G, 1, G, 1)).reshape(G * C, G * P)
    flat = lambda m: m[::C].reshape(1, G * P)
    return flat(abr), flat(abi), blockdiag(bbr).astype(BF16), blockdiag(bbi).astype(BF16)


def _inproj_kernel(x_ref, g_ref, wm_ref, bm_ref, wg_ref, bg_ref, cw_ref, cb_ref,
                   xm_ref, xc_ref, om_ref, us_ref, ga_ref, gb_ref, gt_ref, xbuf_ref, *, widths):
    tm = x_ref.shape[1]
    pad = SUBLANES
    wm, wo, wu, wa, wb = widths

    @pl.when(pl.program_id(1) == 0)
    def _():
        xbuf_ref[0:pad, :] = jnp.zeros((pad, wm), F32)

    hb = _rmsnorm(x_ref[0], g_ref[...]).astype(BF16)

    def proj(c0, w):
        return _dot(hb, wm_ref[:, c0:c0 + w]) + bm_ref[:, c0:c0 + w]

    xm = proj(0, wm)
    xm_ref[0] = xm.astype(BF16)
    xbuf_ref[pad:pad + tm, :] = xm
    acc = _causal_conv(xm, xbuf_ref, cw_ref, cb_ref, 0, wm)
    xbuf_ref[0:pad, :] = xm[tm - pad:tm, :]
    xc_ref[0] = (acc * _sigmoid(acc)).astype(BF16)
    om_ref[0] = _sigmoid(proj(wm, wo)).astype(BF16)
    us_ref[0] = proj(wm + wo, wu).astype(BF16)
    ga_ref[0] = _sigmoid(proj(wm + wo + wu, wa)).astype(BF16)
    gb_ref[0] = _sigmoid(proj(wm + wo + wu + wa, wb)).astype(BF16)
    gt_ref[0] = _dot(hb, wg_ref[...]) + bg_ref[...]


def _inproj(x, norm_g, w_main, b_main, w_gate, b_gate, conv_w, conv_b, widths):
    B, L, D = x.shape
    tm = min(INPROJ_ROWS, L)
    wm, wo, wu, wa, wb = widths
    tok = lambda w: pl.BlockSpec((1, tm, w), lambda b, l: (b, l, 0))
    out_w = (wm, wm, wo, wu, wa, wb)
    out_shape = tuple(jax.ShapeDtypeStruct((B, L, w), BF16) for w in out_w) + (
        jax.ShapeDtypeStruct((B, L, LANES), F32),)
    out_specs = tuple(tok(w) for w in out_w) + (tok(LANES),)
    consts = (norm_g, w_main, b_main, w_gate, b_gate, conv_w, conv_b)
    return pl.pallas_call(
        functools.partial(_inproj_kernel, widths=widths),
        grid=(B, L // tm),
        in_specs=[tok(D)] + [_const_spec(c.shape) for c in consts],
        out_specs=out_specs, out_shape=out_shape,
        scratch_shapes=[pltpu.VMEM((SUBLANES + tm, wm), F32)],
        compiler_params=_params(2), name="inproj",
    )(x, *consts)


def _gelu_tanh(x):
    return 0.5 * x * (1.0 + jnp.tanh(0.7978845608028654 * (x + 0.044715 * x * x * x)))


def _s5_kernel(u_ref, bre_ref, bim_ref, are_ref, aim_ref, cre_ref, cim_ref, d_ref,
               wglu_ref, bglu_ref, o_ref, ut_ref, yt_ref, sr_ref, si_ref, str_ref, sti_ref):
    nb, steps, width = u_ref.shape
    n_state = sr_ref.shape[1]
    k_w = LANES
    n_slabs = width // k_w

    @pl.when(pl.program_id(0) == 0)
    def _():
        str_ref[...] = jnp.zeros_like(str_ref)
        sti_ref[...] = jnp.zeros_like(sti_ref)

    for b in range(nb):
        ub = u_ref[b].astype(F32)
        for q in range(n_slabs):
            ut_ref[q, pl.ds(b, steps, stride=nb), :] = ub[:, q * k_w:(q + 1) * k_w]

    n_w = n_state * k_w // width
    for q in range(n_slabs):
        uq = ut_ref[q].astype(BF16)
        sr_ref[:, q * n_w:(q + 1) * n_w] = _dot(uq, bre_ref[q * k_w:(q + 1) * k_w, q * n_w:(q + 1) * n_w])
        si_ref[:, q * n_w:(q + 1) * n_w] = _dot(uq, bim_ref[q * k_w:(q + 1) * k_w, q * n_w:(q + 1) * n_w])

    for cb in range(n_state // S5_COLS):
        cs = slice(cb * S5_COLS, (cb + 1) * S5_COLS)
        a_r = jnp.broadcast_to(are_ref[:, cs], (SUBLANES, S5_COLS))
        a_i = jnp.broadcast_to(aim_ref[:, cs], (SUBLANES, S5_COLS))

        def body(t, carry, cs=cs, a_r=a_r, a_i=a_i):
            s_r, s_i = carry
            r0 = pl.multiple_of(t * SUBLANES, SUBLANES)
            n_r = a_r * s_r - a_i * s_i + sr_ref[pl.ds(r0, SUBLANES), cs]
            n_i = a_r * s_i + a_i * s_r + si_ref[pl.ds(r0, SUBLANES), cs]
            sr_ref[pl.ds(r0, SUBLANES), cs] = n_r
            si_ref[pl.ds(r0, SUBLANES), cs] = n_i
            return n_r, n_i

        s_r, s_i = lax.fori_loop(0, steps, body, (str_ref[:, cs], sti_ref[:, cs]), unroll=8)
        str_ref[:, cs] = s_r
        sti_ref[:, cs] = s_i

    n_out = MXU_DIM
    k_c = n_state * n_out // width
    halves = []
    for h in range(width // n_out):
        ks = slice(h * k_c, (h + 1) * k_c)
        ns = slice(h * n_out, (h + 1) * n_out)
        halves.append(_dot(sr_ref[:, ks].astype(BF16), cre_ref[ks, ns])
                      - _dot(si_ref[:, ks].astype(BF16), cim_ref[ks, ns]))
    u_tm = jnp.concatenate([ut_ref[q] for q in range(n_slabs)], axis=-1)
    y = jnp.concatenate(halves, axis=-1) + d_ref[...] * u_tm
    y = _gelu_tanh(y)
    z = _dot(y.astype(BF16), wglu_ref[...]) + bglu_ref[...]
    out = y * _sigmoid(z)
    for q in range(n_slabs):
        yt_ref[q] = out[:, q * k_w:(q + 1) * k_w]
    for b in range(nb):
        for q in range(n_slabs):
            o_ref[b, :, q * k_w:(q + 1) * k_w] = yt_ref[q, pl.ds(b, steps, stride=nb), :].astype(o_ref.dtype)


def _s5_mixer(u, bbr, bbi, abr, abi, cbr, cbi, d_row, w_glu, b_glu):
    B, L, width = u.shape
    n_state = abr.shape[1]
    steps = min(S5_STEPS, L)
    rows = steps * B
    blk = pl.BlockSpec((B, steps, width), lambda i: (0, i, 0))
    consts = (bbr, bbi, abr, abi, cbr, cbi, d_row, w_glu, b_glu)
    return pl.pallas_call(
        _s5_kernel,
        grid=(L // steps,),
        in_specs=[blk] + [_const_spec(c.shape) for c in consts],
        out_specs=blk,
        out_shape=jax.ShapeDtypeStruct((B, L, width), BF16),
        scratch_shapes=[pltpu.VMEM((width // LANES, rows, LANES), F32),
                        pltpu.VMEM((width // LANES, rows, LANES), F32),
                        pltpu.VMEM((rows, n_state), F32), pltpu.VMEM((rows, n_state), F32),
                        pltpu.VMEM((B, n_state), F32), pltpu.VMEM((B, n_state), F32)],
        compiler_params=_params(1), name="s5_mixer",
    )(u, *consts)


def _log_sigmoid(x):
    return jnp.minimum(x, 0.0) - jnp.log(1.0 + jnp.exp(-jnp.abs(x)))


def _mlstm_kernel(xm_ref, xc_ref, som_ref, gt_ref, sga_ref, sgb_ref, bo_ref, x_ref,
                  wq_ref, wkt_ref, wv_ref, hg_ref, sk_ref, wa_ref, wb_ref, wo_ref,
                  o_ref, ct_ref, n_ref, m_ref, aout_ref):
    lc = xm_ref.shape[1]
    dh = wq_ref.shape[1]
    n_heads = wq_ref.shape[0]

    @pl.when(pl.program_id(1) == 0)
    def _():
        ct_ref[...] = jnp.zeros_like(ct_ref)
        n_ref[...] = jnp.zeros_like(n_ref)
        m_ref[...] = jnp.zeros_like(m_ref)

    xcb = xc_ref[0]
    xmb = xm_ref[0]

    gt = gt_ref[0]
    lf = _log_sigmoid(gt)
    rid = lax.broadcasted_iota(jnp.int32, (lc, lc), 0)
    cid = lax.broadcasted_iota(jnp.int32, (lc, lc), 1)
    causal = cid <= rid
    tril = jnp.where(causal, 1.0, 0.0).astype(BF16)
    lf_hi = lf.astype(BF16)
    lf_lo = (lf - lf_hi.astype(F32)).astype(BF16)
    bc = _dot(tril, lf_hi) + _dot(tril, lf_lo)
    gt_t = gt.T
    bc_t = bc.T

    scale = dh ** -0.5
    for h in range(n_heads):
        hs = slice(h * dh, (h + 1) * dh)
        q = _dot(xcb[:, hs], wq_ref[h])
        kt = lax.dot_general(wkt_ref[h], xcb[:, hs], (((1,), (1,)), ((), ())),
                             preferred_element_type=F32) * scale
        v = _dot(xmb[:, hs], wv_ref[h])
        ktb, vb = kt.astype(BF16), v.astype(BF16)

        li_row = gt_t[h:h + 1, :]
        b_row = bc_t[n_heads + h:n_heads + h + 1, :]
        b_col = bc[:, n_heads + h:n_heads + h + 1]
        m_prev = m_ref[h, 0:1, 0:1]
        r_row = li_row - b_row

        dmat = jnp.where(causal, b_col + r_row, -jnp.inf)
        inter = b_col + m_prev
        m_row = jnp.maximum(inter, jnp.max(dmat, axis=-1, keepdims=True))
        w_intra = jnp.exp(dmat - m_row)
        w_inter = jnp.exp(inter - m_row)
        s = _dot(q.astype(BF16), ktb) * w_intra
        ct = ct_ref[h]
        lhs = jnp.concatenate([s.astype(BF16), (w_inter * q).astype(BF16)], axis=1)
        rhs = jnp.concatenate([vb, ct.astype(BF16)], axis=0)
        num = _dot(lhs, rhs)
        n_rows = n_ref[h]
        den = (w_inter * jnp.sum(q * n_rows[0:1, :], axis=-1, keepdims=True)
               + jnp.sum(s, axis=-1, keepdims=True))
        hc = num / jnp.maximum(jnp.abs(den), jnp.exp(-m_row))

        b_last = b_col[lc - 1:lc, :]
        g_row = b_last + r_row
        m_new = jnp.maximum(b_last + m_prev, jnp.max(g_row, axis=-1, keepdims=True))
        decay = jnp.exp(b_last + m_prev - m_new)
        wk_row = jnp.exp(g_row - m_new)
        ct_ref[h] = decay * ct + _dot((kt * wk_row).astype(BF16), vb)
        wk_rows = jnp.broadcast_to(wk_row, (SUBLANES, lc)).astype(BF16)
        n_ref[h] = decay * n_rows + lax.dot_general(wk_rows, ktb, (((1,), (1,)), ((), ())),
                                                    preferred_element_type=F32)
        m_ref[h] = jnp.broadcast_to(m_new, m_ref.shape[1:])

        mu = jnp.mean(hc, axis=-1, keepdims=True)
        dv = hc - mu
        var = jnp.mean(dv * dv, axis=-1, keepdims=True)
        hn = dv * lax.rsqrt(var + EPS) * hg_ref[:, hs]
        out_h = (som_ref[0, :, hs].astype(F32) * hn
                 + sk_ref[:, hs] * xc_ref[0, :, hs].astype(F32))
        aout_ref[:, hs] = out_h.astype(BF16)

    pa = _dot(aout_ref[...], wa_ref[...])
    pb = _dot(bo_ref[0], wb_ref[...])
    merged = sga_ref[0].astype(F32) * pa + sgb_ref[0].astype(F32) * pb
    o_ref[0] = x_ref[0] + _dot(merged.astype(BF16), wo_ref[...])


def _mlstm_merge(xm, xc, som, gt, sga, sgb, bo, x, wq, wk, wv, head_g, skip, w_a, w_b, w_out):
    B, L, D = x.shape
    width = xm.shape[-1]
    lc = min(MLSTM_CHUNK, L)
    n_heads, dh, _ = wq.shape
    tok = lambda w: pl.BlockSpec((1, lc, w), lambda b, c: (b, c, 0))
    consts = (wq, wk, wv, head_g, skip, w_a, w_b, w_out)
    return pl.pallas_call(
        _mlstm_kernel,
        grid=(B, L // lc),
        in_specs=[tok(width), tok(width), tok(width), tok(LANES), tok(D), tok(D),
                  tok(bo.shape[-1]), tok(D)]
                 + [_const_spec(c.shape) for c in consts],
        out_specs=tok(D),
        out_shape=jax.ShapeDtypeStruct((B, L, D), F32),
        scratch_shapes=[pltpu.VMEM((n_heads, dh, dh), F32),
                        pltpu.VMEM((n_heads, SUBLANES, dh), F32),
                        pltpu.VMEM((n_heads, SUBLANES, LANES), F32),
                        pltpu.VMEM((lc, width), BF16)],
        compiler_params=_params(2), name="mlstm_merge",
    )(xm, xc, som, gt, sga, sgb, bo, x, *consts)


def _ffn_kernel(x_ref, g_ref, wup_ref, cw_ref, cb_ref, wdn_ref, fg_ref, o_ref,
                up_ref, act_ref, *, final_norm):
    tm = x_ref.shape[1]
    n_up = wup_ref.shape[1]
    f = n_up // 2
    pad = SUBLANES

    @pl.when(pl.program_id(1) == 0)
    def _():
        up_ref[0:pad, :] = jnp.zeros((pad, n_up), F32)

    x = x_ref[0]
    hb = _rmsnorm(x, g_ref[...]).astype(BF16)
    for c0 in range(0, n_up, FFN_COLS):
        up_ref[pad:pad + tm, c0:c0 + FFN_COLS] = _dot(hb, wup_ref[:, c0:c0 + FFN_COLS])

    def conv(c0, w):
        return _causal_conv(up_ref[pad:pad + tm, c0:c0 + w], up_ref, cw_ref, cb_ref, c0, w)

    for c0 in range(0, f, MXU_DIM):
        val = conv(c0, MXU_DIM)
        gate = conv(f + c0, MXU_DIM)
        act_ref[:, c0:c0 + MXU_DIM] = (gate * _sigmoid(gate) * val).astype(BF16)

    up_ref[0:pad, :] = up_ref[tm:tm + pad, :]
    y = x + _dot(act_ref[...], wdn_ref[...])
    if final_norm:
        y = _rmsnorm(y, fg_ref[...])
    o_ref[0] = y


def _ffn(x, norm_g, w_up, conv_w, conv_b, w_down, final_g, final_norm):
    B, L, D = x.shape
    tm = min(FFN_ROWS, L)
    n_up = w_up.shape[1]
    tok = pl.BlockSpec((1, tm, D), lambda b, l: (b, l, 0))
    consts = (norm_g, w_up, conv_w, conv_b, w_down, final_g)
    return pl.pallas_call(
        functools.partial(_ffn_kernel, final_norm=final_norm),
        grid=(B, L // tm),
        in_specs=[tok] + [_const_spec(c.shape) for c in consts],
        out_specs=tok,
        out_shape=jax.ShapeDtypeStruct((B, L, D), F32),
        scratch_shapes=[pltpu.VMEM((SUBLANES + tm, n_up), F32),
                        pltpu.VMEM((tm, n_up // 2), BF16)],
        compiler_params=_params(2), name="ffn",
    )(x, *consts)


def kernel(x, mix_norm_g, w_in, b_in, m_conv_w, m_conv_b, m_wq, m_wk, m_wv, m_head_g, m_skip,
           s_a_re, s_a_im, s_log_dt, s_b_re, s_b_im, s_c_re, s_c_im, s_d, s_w_glu, s_b_glu,
           w_branch_a, w_branch_b, w_out, ffn_norm_g, w_up, ffn_conv_w, ffn_conv_b, w_down,
           final_norm_g):
    B, L, D = x.shape
    assert B == SUBLANES, "s5_mixer lays the batch on the 8 sublanes"
    depth = w_in.shape[0]
    H = m_wq.shape[1]
    mw = m_wq.shape[1] * m_wq.shape[2]
    G, C = s_d.shape[1], s_d.shape[2]
    sw = G * C
    row = lambda v: v.reshape(1, -1).astype(F32)

    for l in range(depth):
        o_ig = 2 * mw
        o_us = o_ig + 2 * H
        o_ga = o_us + sw
        widths = (mw, mw, sw, D, D)
        w_main = jnp.concatenate([w_in[l][:, :o_ig], w_in[l][:, o_us:]], axis=1).astype(BF16)
        b_main = row(jnp.concatenate([b_in[l][:o_ig], b_in[l][o_us:]]))
        w_gate = jnp.pad(w_in[l][:, o_ig:o_us], ((0, 0), (0, LANES - 2 * H))).astype(BF16)
        b_gate = row(jnp.pad(b_in[l][o_ig:o_us], (0, LANES - 2 * H)))

        xm, xc, som, us, sga, sgb, gt = _inproj(x, row(mix_norm_g[l]), w_main, b_main, w_gate, b_gate,
                                                m_conv_w[l].astype(F32), row(m_conv_b[l]), widths)

        abr, abi, bbr, bbi = _s5_prep(s_a_re[l], s_a_im[l], s_log_dt[l], s_b_re[l], s_b_im[l])
        eye = jnp.eye(G, dtype=F32)
        P = s_a_re.shape[-1]
        cdiag = lambda c: (c.transpose(0, 2, 1).reshape(G, P, 1, C)
                           * eye.reshape(G, 1, G, 1)).reshape(G * P, sw).astype(BF16)
        bo = _s5_mixer(us, bbr, bbi, abr, abi, cdiag(s_c_re[l]), cdiag(s_c_im[l]), row(s_d[l]),
                       s_w_glu[l].astype(BF16), row(s_b_glu[l]))

        x = _mlstm_merge(xm, xc, som, gt, sga, sgb, bo, x,
                         m_wq[l].astype(BF16), m_wk[l].transpose(0, 2, 1).astype(BF16), m_wv[l].astype(BF16),
                         row(m_head_g[l]), row(m_skip[l]),
                         w_branch_a[l].astype(BF16), w_branch_b[l].astype(BF16), w_out[l].astype(BF16))

        x = _ffn(x, row(ffn_norm_g[l]), w_up[l].astype(BF16), ffn_conv_w[l].astype(F32),
                 row(ffn_conv_b[l]), w_down[l].astype(BF16), row(final_norm_g),
                 final_norm=(l == depth - 1))
    return x
```

```python
import functools

import jax
import jax.numpy as jnp
from jax import lax
from jax.experimental import pallas as pl
from jax.experimental.pallas import tpu as pltpu

F32 = jnp.float32
BF16 = jnp.bfloat16
EPS = 1e-6

SUBLANES = 8
LANES = 128
MXU_DIM = 256
VMEM_LIMIT_BYTES = 56 * 1024 * 1024

INPROJ_ROWS = 512
INPROJ_COLS = 512
S5_STEPS = 64
S5_COLS = 512
MLSTM_CHUNK = 256
FFN_ROWS = 256


def _const_spec(shape):
    nd = len(shape)
    return pl.BlockSpec(shape, lambda *_: (0,) * nd, pipeline_mode=pl.Buffered(1))


def _params(n_axes):
    return pltpu.CompilerParams(dimension_semantics=("arbitrary",) * n_axes,
                                vmem_limit_bytes=VMEM_LIMIT_BYTES)


def _dot(a, b):
    return jnp.dot(a, b, preferred_element_type=F32)


def _sigmoid(x):
    return 0.5 * jnp.tanh(0.5 * x) + 0.5


def _slab_store(buf_ref, c0, y):
    n = y.shape[0]
    for k in range(y.shape[1] // LANES):
        buf_ref[c0 // LANES + k, SUBLANES:SUBLANES + n, :] = y[:, k * LANES:(k + 1) * LANES]


def _slab_keep_tail(buf_ref, n, c0, width):
    js = slice(c0 // LANES, (c0 + width) // LANES)
    buf_ref[js, 0:SUBLANES, :] = buf_ref[js, n:n + SUBLANES, :]


def _causal_conv(buf_ref, w_ref, b_ref, c0, width, n):
    taps = w_ref.shape[0]
    outs = []
    for j in range(c0 // LANES, (c0 + width) // LANES):
        cs = slice(j * LANES, (j + 1) * LANES)
        acc = b_ref[:, cs]
        for d in range(taps):
            acc = acc + w_ref[taps - 1 - d:taps - d, cs] * buf_ref[j, SUBLANES - d:SUBLANES - d + n, :]
        outs.append(acc)
    return jnp.concatenate(outs, axis=1)


def _rmsnorm(x, g):
    return x * lax.rsqrt(jnp.mean(x * x, axis=-1, keepdims=True) + EPS) * g


def _s5_prep_kernel(ar_ref, ai_ref, ldt_ref, bre_ref, bim_ref,
                    abr_ref, abi_ref, bbr_ref, bbi_ref):
    ar, ai = ar_ref[...], ai_ref[...]
    dt = jnp.exp(ldt_ref[...])
    mag = jnp.exp(dt * ar)
    abar_re = mag * jnp.cos(dt * ai)
    abar_im = mag * jnp.sin(dt * ai)
    den = ar * ar + ai * ai
    xr, xi = abar_re - 1.0, abar_im
    r_re = (xr * ar + xi * ai) / den
    r_im = (xi * ar - xr * ai) / den
    br, bi = bre_ref[...], bim_ref[...]
    abr_ref[...] = abar_re
    abi_ref[...] = abar_im
    bbr_ref[...] = r_re * br - r_im * bi
    bbi_ref[...] = r_re * bi + r_im * br


def _s5_prep(a_re, a_im, log_dt, b_re, b_im):
    G, P = a_re.shape
    C = b_re.shape[-1]
    rep = lambda a: jnp.repeat(a, C, axis=0)
    ar, ai = rep(a_re), rep(a_im)
    ldt = rep(jnp.broadcast_to(log_dt[:, None], (G, P)))
    bt = lambda b: b.transpose(0, 2, 1).reshape(G * C, P)
    shp = jax.ShapeDtypeStruct((G * C, P), F32)
    abr, abi, bbr, bbi = pl.pallas_call(
        _s5_prep_kernel, out_shape=(shp, shp, shp, shp), name="s5_prep",
    )(ar, ai, ldt, bt(b_re), bt(b_im))
    eye = jnp.eye(G, dtype=F32)
    blockdiag = lambda m: (m.reshape(G, C, 1, P) * eye.reshape(G, 1, G, 1)).reshape(G * C, G * P)
    flat = lambda m: m[::C].reshape(1, G * P)
    return flat(abr), flat(abi), blockdiag(bbr).astype(BF16), blockdiag(bbi).astype(BF16)


def _inproj_kernel(x_ref, g_ref, wm_ref, bm_ref, wg_ref, bg_ref, cw_ref, cb_ref,
                   xm_ref, xc_ref, om_ref, us_ref, ga_ref, gb_ref, gt_ref, xbuf_ref, *, widths):
    tm = x_ref.shape[1]
    pad = SUBLANES
    wm, wo, wu, wa, wb = widths

    @pl.when(pl.program_id(1) == 0)
    def _():
        xbuf_ref[:, 0:pad, :] = jnp.zeros((wm // LANES, pad, LANES), F32)

    hb = _rmsnorm(x_ref[0], g_ref[...]).astype(BF16)

    def proj(c0):
        return _dot(hb, wm_ref[:, c0:c0 + INPROJ_COLS]) + bm_ref[:, c0:c0 + INPROJ_COLS]

    def finish_xm(y, oc):
        cs = slice(oc, oc + INPROJ_COLS)
        xm_ref[0, :, cs] = y.astype(BF16)
        _slab_store(xbuf_ref, oc, y)
        acc = _causal_conv(xbuf_ref, cw_ref, cb_ref, oc, INPROJ_COLS, tm)
        _slab_keep_tail(xbuf_ref, tm, oc, INPROJ_COLS)
        xc_ref[0, :, cs] = (acc * _sigmoid(acc)).astype(BF16)

    def finish_plain(o_ref):
        def fn(y, oc):
            o_ref[0, :, oc:oc + INPROJ_COLS] = y.astype(BF16)
        return fn

    def finish_gate(o_ref):
        def fn(y, oc):
            o_ref[0, :, oc:oc + INPROJ_COLS] = _sigmoid(y).astype(BF16)
        return fn

    groups = ((wm, finish_xm), (wo, finish_gate(om_ref)), (wu, finish_plain(us_ref)),
              (wa, finish_gate(ga_ref)), (wb, finish_gate(gb_ref)))
    chunks = []
    col = 0
    for w, fn in groups:
        chunks += [(col + oc, oc, fn) for oc in range(0, w, INPROJ_COLS)]
        col += w
    y = proj(chunks[0][0])
    for i, (_, oc, fn) in enumerate(chunks):
        if i + 1 < len(chunks):
            y_next = proj(chunks[i + 1][0])
        else:
            y_next = _dot(hb, wg_ref[...]) + bg_ref[...]
        fn(y, oc)
        y = y_next
    gt_ref[0] = y


def _inproj(x, norm_g, w_main, b_main, w_gate, b_gate, conv_w, conv_b, widths):
    B, L, D = x.shape
    tm = min(INPROJ_ROWS, L)
    wm, wo, wu, wa, wb = widths
    tok = lambda w: pl.BlockSpec((1, tm, w), lambda b, l: (b, l, 0))
    out_w = (wm, wm, wo, wu, wa, wb)
    out_shape = tuple(jax.ShapeDtypeStruct((B, L, w), BF16) for w in out_w) + (
        jax.ShapeDtypeStruct((B, L, LANES), F32),)
    out_specs = tuple(tok(w) for w in out_w) + (tok(LANES),)
    consts = (norm_g, w_main, b_main, w_gate, b_gate, conv_w, conv_b)
    return pl.pallas_call(
        functools.partial(_inproj_kernel, widths=widths),
        grid=(B, L // tm),
        in_specs=[tok(D)] + [_const_spec(c.shape) for c in consts],
        out_specs=out_specs, out_shape=out_shape,
        scratch_shapes=[pltpu.VMEM((wm // LANES, SUBLANES + tm, LANES), F32)],
        compiler_params=_params(2), name="inproj",
    )(x, *consts)


def _gelu_tanh(x):
    return 0.5 * x * (1.0 + jnp.tanh(0.7978845608028654 * (x + 0.044715 * x * x * x)))


def _s5_kernel(u_ref, bre_ref, bim_ref, are_ref, aim_ref, cre_ref, cim_ref, d_ref,
               wglu_ref, bglu_ref, o_ref, ut_ref, yt_ref, sr_ref, si_ref, str_ref, sti_ref):
    nb, steps, width = u_ref.shape
    n_state = sr_ref.shape[1]
    k_w = LANES
    n_slabs = width // k_w

    @pl.when(pl.program_id(0) == 0)
    def _():
        str_ref[...] = jnp.zeros_like(str_ref)
        sti_ref[...] = jnp.zeros_like(sti_ref)

    for b in range(nb):
        ub = u_ref[b].astype(F32)
        for q in range(n_slabs):
            ut_ref[q, pl.ds(b, steps, stride=nb), :] = ub[:, q * k_w:(q + 1) * k_w]

    n_w = n_state * k_w // width
    for q in range(n_slabs):
        uq = ut_ref[q].astype(BF16)
        sr_ref[:, q * n_w:(q + 1) * n_w] = _dot(uq, bre_ref[q * k_w:(q + 1) * k_w, q * n_w:(q + 1) * n_w])
        si_ref[:, q * n_w:(q + 1) * n_w] = _dot(uq, bim_ref[q * k_w:(q + 1) * k_w, q * n_w:(q + 1) * n_w])

    for cb in range(n_state // S5_COLS):
        cs = slice(cb * S5_COLS, (cb + 1) * S5_COLS)
        a_r = jnp.broadcast_to(are_ref[:, cs], (SUBLANES, S5_COLS))
        a_i = jnp.broadcast_to(aim_ref[:, cs], (SUBLANES, S5_COLS))

        def body(t, carry, cs=cs, a_r=a_r, a_i=a_i):
            s_r, s_i = carry
            r0 = pl.multiple_of(t * SUBLANES, SUBLANES)
            n_r = a_r * s_r - a_i * s_i + sr_ref[pl.ds(r0, SUBLANES), cs]
            n_i = a_r * s_i + a_i * s_r + si_ref[pl.ds(r0, SUBLANES), cs]
            sr_ref[pl.ds(r0, SUBLANES), cs] = n_r
            si_ref[pl.ds(r0, SUBLANES), cs] = n_i
            return n_r, n_i

        s_r, s_i = lax.fori_loop(0, steps, body, (str_ref[:, cs], sti_ref[:, cs]), unroll=8)
        str_ref[:, cs] = s_r
        sti_ref[:, cs] = s_i

    n_out = MXU_DIM
    k_c = n_state * n_out // width
    halves = []
    for h in range(width // n_out):
        ks = slice(h * k_c, (h + 1) * k_c)
        ns = slice(h * n_out, (h + 1) * n_out)
        halves.append(_dot(sr_ref[:, ks].astype(BF16), cre_ref[ks, ns])
                      - _dot(si_ref[:, ks].astype(BF16), cim_ref[ks, ns]))
    u_tm = jnp.concatenate([ut_ref[q] for q in range(n_slabs)], axis=-1)
    y = jnp.concatenate(halves, axis=-1) + d_ref[...] * u_tm
    y = _gelu_tanh(y)
    z = _dot(y.astype(BF16), wglu_ref[...]) + bglu_ref[...]
    out = y * _sigmoid(z)
    for q in range(n_slabs):
        yt_ref[q] = out[:, q * k_w:(q + 1) * k_w]
    for b in range(nb):
        for q in range(n_slabs):
            o_ref[b, :, q * k_w:(q + 1) * k_w] = yt_ref[q, pl.ds(b, steps, stride=nb), :].astype(o_ref.dtype)


def _s5_mixer(u, bbr, bbi, abr, abi, cbr, cbi, d_row, w_glu, b_glu):
    B, L, width = u.shape
    n_state = abr.shape[1]
    steps = min(S5_STEPS, L)
    rows = steps * B
    blk = pl.BlockSpec((B, steps, width), lambda i: (0, i, 0))
    consts = (bbr, bbi, abr, abi, cbr, cbi, d_row, w_glu, b_glu)
    return pl.pallas_call(
        _s5_kernel,
        grid=(L // steps,),
        in_specs=[blk] + [_const_spec(c.shape) for c in consts],
        out_specs=blk,
        out_shape=jax.ShapeDtypeStruct((B, L, width), BF16),
        scratch_shapes=[pltpu.VMEM((width // LANES, rows, LANES), F32),
                        pltpu.VMEM((width // LANES, rows, LANES), F32),
                        pltpu.VMEM((rows, n_state), F32), pltpu.VMEM((rows, n_state), F32),
                        pltpu.VMEM((B, n_state), F32), pltpu.VMEM((B, n_state), F32)],
        compiler_params=_params(1), name="s5_mixer",
    )(u, *consts)


def _log_sigmoid(x):
    return jnp.minimum(x, 0.0) - jnp.log(1.0 + jnp.exp(-jnp.abs(x)))


def _mlstm_kernel(xm_ref, xc_ref, som_ref, gt_ref, sga_ref, sgb_ref, bo_ref, x_ref,
                  wq_ref, wkt_ref, wv_ref, hg_ref, sk_ref, wa_ref, wb_ref, wo_ref,
                  o_ref, ct_ref, n_ref, m_ref, aout_ref, *, n_chunks):
    lc = xm_ref.shape[1]
    dh = wq_ref.shape[1]
    n_heads = wq_ref.shape[0]
    step = pl.program_id(0)
    slot = lax.rem(step, 2)

    @pl.when(lax.rem(step, n_chunks) == 0)
    def _():
        ct_ref[...] = jnp.zeros_like(ct_ref)
        n_ref[...] = jnp.zeros_like(n_ref)
        m_ref[...] = jnp.zeros_like(m_ref)

    @pl.when(step == 0)
    def _():
        aout_ref[1] = jnp.zeros(aout_ref.shape[1:], BF16)

    xcb = xc_ref[0]
    xmb = xm_ref[0]

    gt = gt_ref[0]
    lf = _log_sigmoid(gt)
    rid = lax.broadcasted_iota(jnp.int32, (lc, lc), 0)
    cid = lax.broadcasted_iota(jnp.int32, (lc, lc), 1)
    causal = cid <= rid
    tril = jnp.where(causal, 1.0, 0.0).astype(BF16)
    lf_hi = lf.astype(BF16)
    lf_lo = (lf - lf_hi.astype(F32)).astype(BF16)
    bc = _dot(tril, lf_hi) + _dot(tril, lf_lo)
    gt_t = gt.T
    bc_t = bc.T

    scale = dh ** -0.5
    heads = range(n_heads)
    hsl = [slice(h * dh, (h + 1) * dh) for h in heads]

    q = [_dot(xcb[:, hsl[h]], wq_ref[h]) for h in heads]
    kt = [lax.dot_general(wkt_ref[h], xcb[:, hsl[h]], (((1,), (1,)), ((), ())),
                          preferred_element_type=F32) * scale for h in heads]
    vb = [_dot(xmb[:, hsl[h]], wv_ref[h]).astype(BF16) for h in heads]
    ktb = [kt[h].astype(BF16) for h in heads]

    pa = _dot(aout_ref[1 - slot], wa_ref[...])

    b_col = [bc[:, n_heads + h:n_heads + h + 1] for h in heads]
    r_row = [gt_t[h:h + 1, :] - bc_t[n_heads + h:n_heads + h + 1, :] for h in heads]
    m_prev = [m_ref[h, 0:1, 0:1] for h in heads]
    m_row, w_intra, w_inter = [], [], []
    for h in heads:
        dmat = jnp.where(causal, b_col[h] + r_row[h], -jnp.inf)
        inter = b_col[h] + m_prev[h]
        m_row.append(jnp.maximum(inter, jnp.max(dmat, axis=-1, keepdims=True)))
        w_intra.append(jnp.exp(dmat - m_row[h]))
        w_inter.append(jnp.exp(inter - m_row[h]))

    pb = _dot(bo_ref[0], wb_ref[...])
    merged = (sga_ref[0].astype(F32) * pa + sgb_ref[0].astype(F32) * pb).astype(BF16)

    s = [_dot(q[h].astype(BF16), ktb[h]) * w_intra[h] for h in heads]

    ct = [ct_ref[h] for h in heads]
    n_rows = [n_ref[h] for h in heads]
    hc = []
    for h in heads:
        lhs = jnp.concatenate([s[h].astype(BF16), (w_inter[h] * q[h]).astype(BF16)], axis=1)
        rhs = jnp.concatenate([vb[h], ct[h].astype(BF16)], axis=0)
        num = _dot(lhs, rhs)
        den = (w_inter[h] * jnp.sum(q[h] * n_rows[h][0:1, :], axis=-1, keepdims=True)
               + jnp.sum(s[h], axis=-1, keepdims=True))
        hc.append(num / jnp.maximum(jnp.abs(den), jnp.exp(-m_row[h])))

    half = o_ref.shape[2] // 2
    o_ref[0, :, 0:half] = x_ref[0, :, 0:half] + _dot(merged, wo_ref[:, 0:half])

    for h in heads:
        if h == n_heads // 2:
            o_ref[0, :, half:] = x_ref[0, :, half:] + _dot(merged, wo_ref[:, half:])
        mu = jnp.mean(hc[h], axis=-1, keepdims=True)
        dv = hc[h] - mu
        var = jnp.mean(dv * dv, axis=-1, keepdims=True)
        hn = dv * lax.rsqrt(var + EPS) * hg_ref[:, hsl[h]]
        out_h = (som_ref[0, :, hsl[h]].astype(F32) * hn
                 + sk_ref[:, hsl[h]] * xc_ref[0, :, hsl[h]].astype(F32))
        aout_ref[slot, :, hsl[h]] = out_h.astype(BF16)

        b_last = b_col[h][lc - 1:lc, :]
        g_row = b_last + r_row[h]
        m_new = jnp.maximum(b_last + m_prev[h], jnp.max(g_row, axis=-1, keepdims=True))
        decay = jnp.exp(b_last + m_prev[h] - m_new)
        wk_row = jnp.exp(g_row - m_new)
        ct_ref[h] = decay * ct[h] + _dot((kt[h] * wk_row).astype(BF16), vb[h])
        wk_rows = jnp.broadcast_to(wk_row, (SUBLANES, lc)).astype(BF16)
        n_ref[h] = decay * n_rows[h] + lax.dot_general(wk_rows, ktb[h], (((1,), (1,)), ((), ())),
                                                       preferred_element_type=F32)
        m_ref[h] = jnp.broadcast_to(m_new, m_ref.shape[1:])


def _mlstm_merge(xm, xc, som, gt, sga, sgb, bo, x, wq, wk, wv, head_g, skip, w_a, w_b, w_out):
    B, L, D = x.shape
    width = xm.shape[-1]
    lc = min(MLSTM_CHUNK, L)
    n_heads, dh, _ = wq.shape
    n_chunks = L // lc
    n_steps = B * n_chunks

    def heads_idx(i):
        j = jnp.minimum(i, n_steps - 1)
        return (j // n_chunks, j % n_chunks, 0)

    def merge_idx(i):
        j = jnp.maximum(i - 1, 0)
        return (j // n_chunks, j % n_chunks, 0)

    heads = lambda w: pl.BlockSpec((1, lc, w), heads_idx)
    merge = lambda w: pl.BlockSpec((1, lc, w), merge_idx)
    consts = (wq, wk, wv, head_g, skip, w_a, w_b, w_out)
    return pl.pallas_call(
        functools.partial(_mlstm_kernel, n_chunks=n_chunks),
        grid=(n_steps + 1,),
        in_specs=[heads(width), heads(width), heads(width), heads(LANES), merge(D), merge(D),
                  merge(bo.shape[-1]), merge(D)]
                 + [_const_spec(c.shape) for c in consts],
        out_specs=merge(D),
        out_shape=jax.ShapeDtypeStruct((B, L, D), F32),
        scratch_shapes=[pltpu.VMEM((n_heads, dh, dh), F32),
                        pltpu.VMEM((n_heads, SUBLANES, dh), F32),
                        pltpu.VMEM((n_heads, SUBLANES, LANES), F32),
                        pltpu.VMEM((2, lc, width), BF16)],
        compiler_params=_params(1), name="mlstm_merge",
    )(xm, xc, som, gt, sga, sgb, bo, x, *consts)


def _ffn_kernel(xc_ref, xp_ref, g_ref, wup_ref, cw_ref, cb_ref, wdn_ref, fg_ref, o_ref,
                up_ref, act_ref, *, n_tiles, final_norm):
    tm = xc_ref.shape[1]
    n_up = wup_ref.shape[1]
    f = n_up // 2
    d_out = o_ref.shape[2]
    pad = SUBLANES
    cw = MXU_DIM
    step = pl.program_id(0)
    slot = lax.rem(step, 2)

    @pl.when(lax.rem(step, n_tiles) == 0)
    def _():
        up_ref[:, 0:pad, :] = jnp.zeros((n_up // LANES, pad, LANES), F32)

    @pl.when(step == 0)
    def _():
        act_ref[1] = jnp.zeros(act_ref.shape[1:], BF16)

    hb = _rmsnorm(xc_ref[0], g_ref[...]).astype(BF16)

    def up(c0):
        for c in (c0, f + c0):
            _slab_store(up_ref, c, _dot(hb, wup_ref[:, c:c + cw]))

    def gate(c0):
        val = _causal_conv(up_ref, cw_ref, cb_ref, c0, cw, tm)
        g = _causal_conv(up_ref, cw_ref, cb_ref, f + c0, cw, tm)
        act_ref[slot, :, c0:c0 + cw] = (g * _sigmoid(g) * val).astype(BF16)

    def down(n0):
        o_ref[0, :, n0:n0 + cw] = xp_ref[0, :, n0:n0 + cw] + _dot(act_ref[1 - slot], wdn_ref[:, n0:n0 + cw])

    cols = list(range(0, f, cw))
    downs = list(range(0, d_out, cw))
    every = max(1, len(cols) // len(downs))
    up(cols[0])
    for i, c0 in enumerate(cols):
        if i + 1 < len(cols):
            up(cols[i + 1])
        gate(c0)
        if i % every == every - 1 and downs:
            down(downs.pop(0))
    for n0 in downs:
        down(n0)

    _slab_keep_tail(up_ref, tm, 0, n_up)
    if final_norm:
        o_ref[0] = _rmsnorm(o_ref[0], fg_ref[...])


def _ffn(x, norm_g, w_up, conv_w, conv_b, w_down, final_g, final_norm):
    B, L, D = x.shape
    tm = min(FFN_ROWS, L)
    n_up = w_up.shape[1]
    n_tiles = L // tm
    n_steps = B * n_tiles

    def cur_idx(i):
        j = jnp.minimum(i, n_steps - 1)
        return (j // n_tiles, j % n_tiles, 0)

    def prev_idx(i):
        j = jnp.maximum(i - 1, 0)
        return (j // n_tiles, j % n_tiles, 0)

    consts = (norm_g, w_up, conv_w, conv_b, w_down, final_g)
    return pl.pallas_call(
        functools.partial(_ffn_kernel, n_tiles=n_tiles, final_norm=final_norm),
        grid=(n_steps + 1,),
        in_specs=[pl.BlockSpec((1, tm, D), cur_idx), pl.BlockSpec((1, tm, D), prev_idx)]
                 + [_const_spec(c.shape) for c in consts],
        out_specs=pl.BlockSpec((1, tm, D), prev_idx),
        out_shape=jax.ShapeDtypeStruct((B, L, D), F32),
        scratch_shapes=[pltpu.VMEM((n_up // LANES, SUBLANES + tm, LANES), F32),
                        pltpu.VMEM((2, tm, n_up // 2), BF16)],
        compiler_params=_params(1), name="ffn",
    )(x, x, *consts)


def kernel(x, mix_norm_g, w_in, b_in, m_conv_w, m_conv_b, m_wq, m_wk, m_wv, m_head_g, m_skip,
           s_a_re, s_a_im, s_log_dt, s_b_re, s_b_im, s_c_re, s_c_im, s_d, s_w_glu, s_b_glu,
           w_branch_a, w_branch_b, w_out, ffn_norm_g, w_up, ffn_conv_w, ffn_conv_b, w_down,
           final_norm_g):
    B, L, D = x.shape
    assert B == SUBLANES, "s5_mixer lays the batch on the 8 sublanes"
    depth = w_in.shape[0]
    H = m_wq.shape[1]
    mw = m_wq.shape[1] * m_wq.shape[2]
    G, C = s_d.shape[1], s_d.shape[2]
    sw = G * C
    row = lambda v: v.reshape(1, -1).astype(F32)

    for l in range(depth):
        o_ig = 2 * mw
        o_us = o_ig + 2 * H
        o_ga = o_us + sw
        widths = (mw, mw, sw, D, D)
        w_main = jnp.concatenate([w_in[l][:, :o_ig], w_in[l][:, o_us:]], axis=1).astype(BF16)
        b_main = row(jnp.concatenate([b_in[l][:o_ig], b_in[l][o_us:]]))
        w_gate = jnp.pad(w_in[l][:, o_ig:o_us], ((0, 0), (0, LANES - 2 * H))).astype(BF16)
        b_gate = row(jnp.pad(b_in[l][o_ig:o_us], (0, LANES - 2 * H)))

        xm, xc, som, us, sga, sgb, gt = _inproj(x, row(mix_norm_g[l]), w_main, b_main, w_gate, b_gate,
                                                m_conv_w[l].astype(F32), row(m_conv_b[l]), widths)

        abr, abi, bbr, bbi = _s5_prep(s_a_re[l], s_a_im[l], s_log_dt[l], s_b_re[l], s_b_im[l])
        eye = jnp.eye(G, dtype=F32)
        P = s_a_re.shape[-1]
        cdiag = lambda c: (c.transpose(0, 2, 1).reshape(G, P, 1, C)
                           * eye.reshape(G, 1, G, 1)).reshape(G * P, sw).astype(BF16)
        bo = _s5_mixer(us, bbr, bbi, abr, abi, cdiag(s_c_re[l]), cdiag(s_c_im[l]), row(s_d[l]),
                       s_w_glu[l].astype(BF16), row(s_b_glu[l]))

        x = _mlstm_merge(xm, xc, som, gt, sga, sgb, bo, x,
                         m_wq[l].astype(BF16), m_wk[l].transpose(0, 2, 1).astype(BF16), m_wv[l].astype(BF16),
                         row(m_head_g[l]), row(m_skip[l]),
                         w_branch_a[l].astype(BF16), w_branch_b[l].astype(BF16), w_out[l].astype(BF16))

        x = _ffn(x, row(ffn_norm_g[l]), w_up[l].astype(BF16), ffn_conv_w[l].astype(F32),
                 row(ffn_conv_b[l]), w_down[l].astype(BF16), row(final_norm_g),
                 final_norm=(l == depth - 1))
    return x
```
